```python
import jax, jax.numpy as jnp
from jax import lax
import numpy as np

D_MODEL = 2048
BATCH = 2
SEQ = 4096
DEPTH = 1

HEAD_DIM = 64
N_Q_HEADS = 16
N_KV_HEADS = 4
GQA_GROUP = N_Q_HEADS // N_KV_HEADS
ATTN_WIDTH = N_Q_HEADS * HEAD_DIM
KV_WIDTH = N_KV_HEADS * HEAD_DIM
WINDOW = 128
ATTN_BLOCK = 128
ROPE_THETA = 500000.0
ROT_DIM = HEAD_DIM // 4
POOL_WINDOWS = (2, 4, 8, 16)
N_POOL_GROUPS = len(POOL_WINDOWS)
POOL_WIDTH = D_MODEL // 2
POOL_GROUP_WIDTH = POOL_WIDTH // N_POOL_GROUPS
MIX_WIDTH = ATTN_WIDTH + POOL_WIDTH
IN_WIDTH = ATTN_WIDTH + 2 * KV_WIDTH + POOL_WIDTH
PEER_HEADS = 8
PEER_KEY_DIM = 256
PEER_HALF = PEER_KEY_DIM // 2
PEER_N_KEYS = 128
PEER_N_EXPERTS = PEER_N_KEYS * PEER_N_KEYS
PEER_TOPK = 16
PEER_CHUNK = 128
PLE_DIM = 256
RMS_EPS = 1e-6

kernel_name = "hymba_swa_pool_peer_block"


def rmsnorm(x, g):
    xf = x.astype(jnp.float32)
    y = xf * lax.rsqrt(jnp.mean(xf * xf, axis=-1, keepdims=True) + RMS_EPS)
    return (y * g.astype(jnp.float32)).astype(x.dtype)


def rope_tables(positions):
    inv_freq = 1.0 / (ROPE_THETA ** (jnp.arange(0, ROT_DIM, 2, dtype=jnp.float32) / ROT_DIM))
    ang = positions.astype(jnp.float32)[..., None] * inv_freq
    return jnp.cos(ang)[:, :, None, :], jnp.sin(ang)[:, :, None, :]


def apply_partial_rope(x, cos, sin):
    half = ROT_DIM // 2
    xr = x[..., :ROT_DIM].astype(jnp.float32)
    x1, x2 = xr[..., :half], xr[..., half:]
    rot = jnp.concatenate([x1 * cos - x2 * sin, x2 * cos + x1 * sin], axis=-1).astype(x.dtype)
    return jnp.concatenate([rot, x[..., ROT_DIM:]], axis=-1)


def sliding_window_attention(q, k, v, sinks):
    B, S = q.shape[0], q.shape[1]
    nb = S // ATTN_BLOCK
    qb = q.reshape(B, nb, ATTN_BLOCK, N_KV_HEADS, GQA_GROUP, HEAD_DIM)
    kb = k.reshape(B, nb, ATTN_BLOCK, N_KV_HEADS, HEAD_DIM)
    vb = v.reshape(B, nb, ATTN_BLOCK, N_KV_HEADS, HEAD_DIM)
    pad = ((0, 0), (1, 0), (0, 0), (0, 0), (0, 0))
    k_band = jnp.concatenate([jnp.pad(kb, pad)[:, :-1], kb], axis=2)
    v_band = jnp.concatenate([jnp.pad(vb, pad)[:, :-1], vb], axis=2)
    scale = HEAD_DIM ** -0.5
    s = jnp.einsum('bnqhgd,bnshd->bnhgqs', qb, k_band).astype(jnp.float32) * scale
    qi = jnp.arange(ATTN_BLOCK)[:, None] + ATTN_BLOCK
    kj = jnp.arange(2 * ATTN_BLOCK)[None, :]
    rel = qi - kj
    band = (rel >= 0) & (rel < WINDOW)
    has_prev = (jnp.arange(nb) > 0)[:, None, None] | (kj >= ATTN_BLOCK)[None]
    mask = band[None] & has_prev
    s = jnp.where(mask[None, :, None, None], s, jnp.finfo(jnp.float32).min)
    sink = sinks.astype(jnp.float32).reshape(N_KV_HEADS, GQA_GROUP)[None, None, :, :, None, None]
    sink = jnp.broadcast_to(sink, s.shape[:-1] + (1,))
    probs = jax.nn.softmax(jnp.concatenate([s, sink], axis=-1), axis=-1)[..., :-1]
    o = jnp.einsum('bnhgqs,bnshd->bnqhgd', probs.astype(v.dtype), v_band)
    return o.reshape(B, S, N_Q_HEADS * HEAD_DIM)


def multiscale_pool(u, w_pool, pool_scale):
    B, S = u.shape[0], u.shape[1]
    ug = u.reshape(B, S, N_POOL_GROUPS, POOL_GROUP_WIDTH)
    t = jnp.arange(S)
    outs = []
    for gi, w in enumerate(POOL_WINDOWS):
        xg = ug[:, :, gi, :].astype(jnp.float32)
        cs = jnp.cumsum(xg, axis=1)
        shifted = jnp.pad(cs, ((0, 0), (w, 0), (0, 0)))[:, :S]
        count = jnp.minimum(t + 1, w).astype(jnp.float32)[None, :, None]
        outs.append((cs - shifted) / count - xg)
    pooled = jnp.stack(outs, axis=2).astype(u.dtype)
    mixed = jnp.einsum('bsgc,gce->bsge', pooled, w_pool)
    return mixed.reshape(B, S, POOL_WIDTH) * pool_scale


def peer_ffn(xn, w_query, sub_keys, expert_u, expert_v):
    B, S, D = xn.shape
    q = (xn @ w_query).reshape(B, S, PEER_HEADS, PEER_KEY_DIM).astype(jnp.float32)
    sk = sub_keys.astype(jnp.float32)
    s1 = jnp.einsum('bshd,nd->bshn', q[..., :PEER_HALF], sk[0])
    s2 = jnp.einsum('bshd,nd->bshn', q[..., PEER_HALF:], sk[1])
    v1, i1 = lax.top_k(s1, PEER_TOPK)
    v2, i2 = lax.top_k(s2, PEER_TOPK)
    cand = (v1[..., :, None] + v2[..., None, :]).reshape(B, S, PEER_HEADS, PEER_TOPK * PEER_TOPK)
    cidx = (i1[..., :, None] * PEER_N_KEYS + i2[..., None, :]).reshape(B, S, PEER_HEADS, PEER_TOPK * PEER_TOPK)
    top, pos = lax.top_k(cand, PEER_TOPK)
    eidx = jnp.take_along_axis(cidx, pos, axis=-1)
    gates = jax.nn.softmax(top, axis=-1).astype(xn.dtype)
    T = B * S
    nc = T // PEER_CHUNK
    K = PEER_HEADS * PEER_TOPK
    xt = xn.reshape(nc, PEER_CHUNK, D)
    it = eidx.reshape(nc, PEER_CHUNK, K)
    gt = gates.reshape(nc, PEER_CHUNK, K)

    def expert_block(args):
        xc, ic, gc = args
        u = jnp.take(expert_u, ic, axis=0)
        h = jnp.einsum('cd,ckd->ck', xc, u)
        a = gc * jax.nn.gelu(h, approximate=False)
        v = jnp.take(expert_v, ic, axis=0)
        return jnp.einsum('ck,ckd->cd', a, v)

    out = lax.map(expert_block, (xt, it, gt))
    return out.reshape(B, S, D)


def setup_inputs(seed: int = 0) -> dict:
    key = jax.random.key(seed)
    ks = jax.random.split(key, 20)
    f32 = jnp.float32
    n = lambda k, shape, s: jax.random.normal(k, shape, f32) * s
    x = jax.random.normal(ks[0], (BATCH, SEQ, D_MODEL), f32)
    p = jax.random.normal(ks[1], (DEPTH, BATCH, SEQ, PLE_DIM), f32)
    offsets = jax.random.randint(ks[2], (BATCH, 1), 0, 1024, dtype=jnp.int32)
    positions = (offsets + jnp.arange(SEQ, dtype=jnp.int32)[None, :]).astype(jnp.int32)
    return {
        "x": x,
        "p": p,
        "positions": positions,
        "g_mix": 1.0 + n(ks[3], (DEPTH, D_MODEL), 0.05),
        "w_in": n(ks[4], (DEPTH, D_MODEL, IN_WIDTH), D_MODEL ** -0.5),
        "sinks": n(ks[5], (DEPTH, N_Q_HEADS), 0.5),
        "w_pool": n(ks[6], (DEPTH, N_POOL_GROUPS, POOL_GROUP_WIDTH, POOL_GROUP_WIDTH), POOL_GROUP_WIDTH ** -0.5),
        "pool_scale": 1.0 + n(ks[7], (DEPTH, POOL_WIDTH), 0.1),
        "w_out": n(ks[8], (DEPTH, MIX_WIDTH, D_MODEL), MIX_WIDTH ** -0.5),
        "g_ffn": 1.0 + n(ks[9], (DEPTH, D_MODEL), 0.05),
        "w_query": n(ks[10], (DEPTH, D_MODEL, PEER_HEADS * PEER_KEY_DIM), D_MODEL ** -0.5),
        "sub_keys": n(ks[11], (DEPTH, 2, PEER_N_KEYS, PEER_HALF), PEER_HALF ** -0.5),
        "expert_u": n(ks[12], (DEPTH, PEER_N_EXPERTS, D_MODEL), D_MODEL ** -0.5),
        "expert_v": n(ks[13], (DEPTH, PEER_N_EXPERTS, D_MODEL), 0.5),
        "g_ple": 1.0 + n(ks[14], (DEPTH, D_MODEL), 0.05),
        "w_ple_gate": n(ks[15], (DEPTH, D_MODEL, D_MODEL), D_MODEL ** -0.5),
        "w_ple_proj": n(ks[16], (DEPTH, PLE_DIM, D_MODEL), PLE_DIM ** -0.5),
        "g_final": 1.0 + n(ks[17], (D_MODEL,), 0.05),
    }


def reference(x, p, positions, g_mix, w_in, sinks, w_pool, pool_scale, w_out, g_ffn,
              w_query, sub_keys, expert_u, expert_v, g_ple, w_ple_gate, w_ple_proj, g_final):
    B, S = x.shape[0], x.shape[1]
    cos, sin = rope_tables(positions)
    h = x
    for i in range(DEPTH):
        hn = rmsnorm(h, g_mix[i])
        proj = hn @ w_in[i]
        q = proj[..., :ATTN_WIDTH].reshape(B, S, N_Q_HEADS, HEAD_DIM)
        k = proj[..., ATTN_WIDTH:ATTN_WIDTH + KV_WIDTH].reshape(B, S, N_KV_HEADS, HEAD_DIM)
        v = proj[..., ATTN_WIDTH + KV_WIDTH:ATTN_WIDTH + 2 * KV_WIDTH].reshape(B, S, N_KV_HEADS, HEAD_DIM)
        u = proj[..., ATTN_WIDTH + 2 * KV_WIDTH:]
        q = apply_partial_rope(q, cos, sin)
        k = apply_partial_rope(k, cos, sin)
        attn = sliding_window_attention(q, k, v, sinks[i])
        pool = multiscale_pool(u, w_pool[i], pool_scale[i])
        h = h + jnp.concatenate([attn, pool], axis=-1) @ w_out[i]
        h = h + peer_ffn(rmsnorm(h, g_ffn[i]), w_query[i], sub_keys[i], expert_u[i], expert_v[i])
        gate = jax.nn.sigmoid(rmsnorm(h, g_ple[i]) @ w_ple_gate[i])
        h = h + gate * (p[i] @ w_ple_proj[i])
    return rmsnorm(h, g_final)
```

```python
import functools

import jax
import jax.numpy as jnp
import numpy as np
from jax import lax
from jax.experimental import pallas as pl
from jax.experimental.pallas import tpu as pltpu

F32 = jnp.float32
BF16 = jnp.bfloat16

HEAD_DIM = 64
N_Q_HEADS = 16
N_KV_HEADS = 4
GQA_GROUP = N_Q_HEADS // N_KV_HEADS
ATTN_WIDTH = N_Q_HEADS * HEAD_DIM
KV_WIDTH = N_KV_HEADS * HEAD_DIM
ATTN_BLOCK = 128
ROPE_THETA = 500000.0
ROT_DIM = HEAD_DIM // 4
POOL_WINDOWS = (2, 4, 8, 16)
POOL_HALO = 16
PEER_HEADS = 8
PEER_HALF = 128
PEER_N_KEYS = 128
PEER_TOPK = 16
RMS_EPS = 1e-6

LANES = 128
VMEM_LIMIT = 56 * 1024 * 1024

TOK_BLOCK = 512
ROUTE_BLOCK = 256
PEER_TOK_BLOCK = 512
PEER_EXP_BLOCK = 512
PREP_BLOCK = 512

NEG_INF = float("-inf")
POS_INF = float("inf")
MASK_MIN = float(np.finfo(np.float32).min)
SQRT_HALF = float(np.sqrt(0.5).astype(np.float32))


def _params(*semantics):
    return pltpu.CompilerParams(dimension_semantics=semantics, vmem_limit_bytes=VMEM_LIMIT)


def _rmsnorm(x, g):
    ms = jnp.mean(x * x, axis=-1, keepdims=True)
    return x * lax.rsqrt(ms + RMS_EPS) * g


def _dot(a, b):
    return jnp.dot(a, b, preferred_element_type=F32)


def _dot_nt(a, b):
    return lax.dot_general(a, b, (((1,), (1,)), ((), ())), preferred_element_type=F32)


def _prep_kernel(u_ref, v_ref, ub_ref, vt_ref):
    ub_ref[...] = u_ref[...].astype(BF16)
    vt_ref[...] = v_ref[...].T.astype(BF16)


def _prep_experts(expert_u, expert_v):
    n, d = expert_u.shape
    return pl.pallas_call(
        _prep_kernel,
        grid=(n // PREP_BLOCK,),
        in_specs=[pl.BlockSpec((PREP_BLOCK, d), lambda i: (i, 0)),
                  pl.BlockSpec((PREP_BLOCK, d), lambda i: (i, 0))],
        out_specs=[pl.BlockSpec((PREP_BLOCK, d), lambda i: (i, 0)),
                   pl.BlockSpec((d, PREP_BLOCK), lambda i: (0, i))],
        out_shape=[jax.ShapeDtypeStruct((n, d), BF16), jax.ShapeDtypeStruct((d, n), BF16)],
        compiler_params=_params("arbitrary"),
        name="peer_prep",
    )(expert_u, expert_v)


def _rope(t, cos, sin_hi, sin_lo):
    return t * cos + pltpu.roll(t, 8, 1) * sin_hi + pltpu.roll(t, LANES - 8, 1) * sin_lo


def _inproj_kernel(x_ref, g_ref, w_ref, cos_ref, shi_ref, slo_ref, q_ref, k_ref, v_ref, u_ref):
    hn = _rmsnorm(x_ref[...], g_ref[...]).astype(BF16)
    proj = _dot(hn, w_ref[...])
    cos, shi, slo = cos_ref[...], shi_ref[...], slo_ref[...]
    scale = HEAD_DIM ** -0.5
    for j in range(ATTN_WIDTH // LANES):
        t = proj[:, j * LANES:(j + 1) * LANES]
        q_ref[:, j * LANES:(j + 1) * LANES] = (_rope(t, cos, shi, slo) * scale).astype(BF16)
    lane = lax.broadcasted_iota(jnp.int32, (proj.shape[0], LANES), 1)
    lo = lane < HEAD_DIM

    def expand(t, out_ref, m):
        r = pltpu.roll(t, HEAD_DIM, 1)
        tiles = (jnp.where(lo, t, 0.0), jnp.where(lo, 0.0, r), jnp.where(lo, r, 0.0), jnp.where(lo, 0.0, t))
        for n, tile in enumerate(tiles):
            c = (4 * m + n) * LANES
            out_ref[:, c:c + LANES] = tile.astype(BF16)

    for m in range(KV_WIDTH // LANES):
        c = ATTN_WIDTH + m * LANES
        expand(_rope(proj[:, c:c + LANES], cos, shi, slo), k_ref, m)
        c = ATTN_WIDTH + KV_WIDTH + m * LANES
        expand(proj[:, c:c + LANES], v_ref, m)
    u_ref[...] = proj[:, ATTN_WIDTH + 2 * KV_WIDTH:]


def _inproj(x, g, w, cos, shi, slo):
    t, d = x.shape
    n_in = w.shape[1]
    pool_w = n_in - ATTN_WIDTH - 2 * KV_WIDTH
    kx = 4 * KV_WIDTH
    row = lambda i: (i, 0)
    fixed = lambda i: (0, 0)
    return pl.pallas_call(
        _inproj_kernel,
        grid=(t // TOK_BLOCK,),
        in_specs=[pl.BlockSpec((TOK_BLOCK, d), row),
                  pl.BlockSpec((1, d), fixed),
                  pl.BlockSpec((d, n_in), fixed),
                  pl.BlockSpec((TOK_BLOCK, LANES), row),
                  pl.BlockSpec((TOK_BLOCK, LANES), row),
                  pl.BlockSpec((TOK_BLOCK, LANES), row)],
        out_specs=[pl.BlockSpec((TOK_BLOCK, ATTN_WIDTH), row),
                   pl.BlockSpec((TOK_BLOCK, kx), row),
                   pl.BlockSpec((TOK_BLOCK, kx), row),
                   pl.BlockSpec((TOK_BLOCK, pool_w), row)],
        out_shape=[jax.ShapeDtypeStruct((t, ATTN_WIDTH), BF16),
                   jax.ShapeDtypeStruct((t, kx), BF16),
                   jax.ShapeDtypeStruct((t, kx), BF16),
                   jax.ShapeDtypeStruct((t, pool_w), F32)],
        compiler_params=_params("arbitrary"),
        name="inproj",
    )(x, g, w, cos, shi, slo)


def _mix_kernel(sinks_ref, q_ref, k_ref, kh_ref, v_ref, vh_ref, u_ref, uh_ref, x_ref, wp_ref, ps_ref, wo_ref,
                o_ref, cat_ref, *, blocks_per_seq):
    i = pl.program_id(0)
    first = (i % blocks_per_seq) == 0
    tb = q_ref.shape[0]
    nq = tb // ATTN_BLOCK
    qi = lax.broadcasted_iota(jnp.int32, (ATTN_BLOCK, 2 * ATTN_BLOCK), 0)
    kj = lax.broadcasted_iota(jnp.int32, (ATTN_BLOCK, 2 * ATTN_BLOCK), 1)
    band = (kj > qi) & (kj <= qi + ATTN_BLOCK)
    band_first = band & (kj >= ATTN_BLOCK * first.astype(jnp.int32))

    for n in range(nq):
        r0 = n * ATTN_BLOCK
        mask = band_first if n == 0 else band
        for h in range(N_KV_HEADS):
            c0 = h * 2 * LANES
            q2 = jnp.concatenate([q_ref[r0:r0 + ATTN_BLOCK, c0:c0 + LANES],
                                  q_ref[r0:r0 + ATTN_BLOCK, c0 + LANES:c0 + 2 * LANES]], axis=0)
            scores, values = [], []
            for half in range(2):
                c = c0 + half * LANES
                if n == 0:
                    kp, vp = kh_ref[:, c:c + LANES], vh_ref[:, c:c + LANES]
                else:
                    kp = k_ref[r0 - ATTN_BLOCK:r0, c:c + LANES]
                    vp = v_ref[r0 - ATTN_BLOCK:r0, c:c + LANES]
                kb = jnp.concatenate([kp, k_ref[r0:r0 + ATTN_BLOCK, c:c + LANES]], axis=0)
                values.append(jnp.concatenate([vp, v_ref[r0:r0 + ATTN_BLOCK, c:c + LANES]], axis=0))
                scores.append(_dot_nt(q2, kb))
            probs = [[None, None], [None, None]]
            for half in range(2):
                for pair in range(2):
                    head = h * GQA_GROUP + 2 * pair + half
                    s = scores[half][pair * ATTN_BLOCK:(pair + 1) * ATTN_BLOCK]
                    s = jnp.where(mask, s, MASK_MIN)
                    sink = sinks_ref[head]
                    m = jnp.maximum(jnp.max(s, axis=-1, keepdims=True), sink)
                    e = jnp.exp(s - m)
                    den = jnp.sum(e, axis=-1, keepdims=True) + jnp.exp(sink - m)
                    probs[half][pair] = (e / den).astype(BF16)
            o = (_dot(jnp.concatenate(probs[0], axis=0), values[0])
                 + _dot(jnp.concatenate(probs[1], axis=0), values[1]))
            cat_ref[r0:r0 + ATTN_BLOCK, c0:c0 + LANES] = o[:ATTN_BLOCK].astype(BF16)
            cat_ref[r0:r0 + ATTN_BLOCK, c0 + LANES:c0 + 2 * LANES] = o[ATTN_BLOCK:].astype(BF16)

    gw = wp_ref.shape[1]
    t_seq = (i % blocks_per_seq) * tb + lax.broadcasted_iota(jnp.int32, (tb, gw), 0)
    for gi, w in enumerate(POOL_WINDOWS):
        cols = slice(gi * gw, (gi + 1) * gw)
        halo = jnp.where(first, 0.0, uh_ref[:, cols])
        cur = u_ref[:, cols]
        s = jnp.concatenate([halo, cur], axis=0)
        k = 1
        while k < w:
            s = s + pltpu.roll(s, k, 0)
            k *= 2
        count = jnp.minimum(t_seq + 1, w).astype(F32)
        pooled = (s[POOL_HALO:] / count - cur).astype(BF16)
        mixed = _dot(pooled, wp_ref[gi]) * ps_ref[:, cols]
        cat_ref[:, ATTN_WIDTH + gi * gw:ATTN_WIDTH + (gi + 1) * gw] = mixed.astype(BF16)

    o_ref[...] = x_ref[...] + _dot(cat_ref[...], wo_ref[...])


def _mix(sinks, q, kx, vx, u, x, w_pool, pool_scale, w_out, seq):
    t, d = x.shape
    tb = TOK_BLOCK
    halo_k = tb // ATTN_BLOCK
    halo_u = tb // POOL_HALO
    row = lambda i: (i, 0)
    prev_k = lambda i: (jnp.maximum(i * halo_k - 1, 0), 0)
    prev_u = lambda i: (jnp.maximum(i * halo_u - 1, 0), 0)
    fixed2 = lambda i: (0, 0)
    fixed3 = lambda i: (0, 0, 0)
    return pl.pallas_call(
        functools.partial(_mix_kernel, blocks_per_seq=seq // tb),
        grid=(t // tb,),
        in_specs=[pl.BlockSpec(memory_space=pltpu.SMEM),
                  pl.BlockSpec((tb, q.shape[1]), row),
                  pl.BlockSpec((tb, kx.shape[1]), row),
                  pl.BlockSpec((ATTN_BLOCK, kx.shape[1]), prev_k),
                  pl.BlockSpec((tb, vx.shape[1]), row),
                  pl.BlockSpec((ATTN_BLOCK, vx.shape[1]), prev_k),
                  pl.BlockSpec((tb, u.shape[1]), row),
                  pl.BlockSpec((POOL_HALO, u.shape[1]), prev_u),
                  pl.BlockSpec((tb, d), row),
                  pl.BlockSpec(w_pool.shape, fixed3),
                  pl.BlockSpec((1, pool_scale.shape[1]), fixed2),
                  pl.BlockSpec(w_out.shape, fixed2)],
        out_specs=pl.BlockSpec((tb, d), row),
        out_shape=jax.ShapeDtypeStruct((t, d), F32),
        scratch_shapes=[pltpu.VMEM((tb, w_out.shape[0]), BF16)],
        compiler_params=_params("arbitrary"),
        name="mix",
    )(sinks, q, kx, kx, vx, vx, u, u, x, w_pool, pool_scale, w_out)


def _top_values(s, k):
    rows = []
    for _ in range(k):
        m = jnp.max(s, axis=0, keepdims=True)
        rows.append(m)
        s = jnp.where(s == m, NEG_INF, s)
    return rows


def _route_kernel(h_ref, g_ref, wq_ref, sk_ref, xnt_ref, s2_ref, e2_ref, c_ref, w_ref):
    xn = _rmsnorm(h_ref[...], g_ref[...])
    xnt_ref[...] = xn.T.astype(BF16)
    q = _dot(xn.astype(BF16), wq_ref[...]).astype(BF16)
    tb = q.shape[0]
    row16 = lax.broadcasted_iota(jnp.int32, (PEER_TOPK, tb), 0)
    for h in range(PEER_HEADS):
        c0 = h * 2 * PEER_HALF
        s1 = _dot_nt(sk_ref[0], q[:, c0:c0 + PEER_HALF])
        s2 = _dot_nt(sk_ref[1], q[:, c0 + PEER_HALF:c0 + 2 * PEER_HALF])
        v1 = _top_values(s1, PEER_TOPK)
        v2 = _top_values(s2, PEER_TOPK)
        v2_arr = jnp.zeros((PEER_TOPK, tb), F32)
        for r in range(PEER_TOPK):
            v2_arr = jnp.where(row16 == r, v2[r], v2_arr)
        cand = jnp.concatenate([v1[r] + v2_arr for r in range(PEER_TOPK)], axis=0)
        thr = _top_values(cand, PEER_TOPK)[-1]
        top = v1[0] + v2[0]
        z = jnp.sum(jnp.where(cand >= thr, jnp.exp(cand - top), 0.0), axis=0, keepdims=True)
        c = jnp.full(s1.shape, POS_INF, F32)
        for r in range(PEER_TOPK):
            c = jnp.where(s1 + v2[r] >= thr, v2[r], c)
        s2_ref[h] = s2
        e2_ref[h] = jnp.exp(s2 - v2[0])
        c_ref[h] = c
        w_ref[h] = jnp.exp(s1 - v1[0]) * (1.0 / z)


def _route(h1, g, w_query, sub_keys):
    t, d = h1.shape
    tb = ROUTE_BLOCK
    heads = (PEER_HEADS, PEER_N_KEYS, t)
    head_spec = pl.BlockSpec((PEER_HEADS, PEER_N_KEYS, tb), lambda i: (0, 0, i))
    return pl.pallas_call(
        _route_kernel,
        grid=(t // tb,),
        in_specs=[pl.BlockSpec((tb, d), lambda i: (i, 0)),
                  pl.BlockSpec((1, d), lambda i: (0, 0)),
                  pl.BlockSpec(w_query.shape, lambda i: (0, 0)),
                  pl.BlockSpec(sub_keys.shape, lambda i: (0, 0, 0))],
        out_specs=[pl.BlockSpec((d, tb), lambda i: (0, i)), head_spec, head_spec, head_spec, head_spec],
        out_shape=[jax.ShapeDtypeStruct((d, t), BF16)] + [jax.ShapeDtypeStruct(heads, F32)] * 4,
        compiler_params=_params("arbitrary"),
        name="peer_route",
    )(h1, g, w_query, sub_keys)


def _peer_kernel(xnt_ref, u_ref, vt_ref, s2_ref, e2_ref, c_ref, w_ref, o_ref):
    e = pl.program_id(1)

    @pl.when(e == 0)
    def _():
        o_ref[...] = jnp.zeros_like(o_ref)

    ht = _dot(u_ref[...], xnt_ref[...])
    n_first = u_ref.shape[0] // PEER_N_KEYS
    acts = []
    for ai in range(n_first):
        a = e * n_first + ai
        gate = None
        for h in range(PEER_HEADS):
            sel = s2_ref[h] >= c_ref[h, pl.ds(a, 1), :]
            term = jnp.where(sel, e2_ref[h], 0.0) * w_ref[h, pl.ds(a, 1), :]
            gate = term if gate is None else gate + term
        hh = ht[ai * PEER_N_KEYS:(ai + 1) * PEER_N_KEYS]
        gelu = 0.5 * hh * (1.0 + lax.erf(hh * SQRT_HALF))
        acts.append((gelu * gate).astype(BF16))
    o_ref[...] += _dot(vt_ref[...], jnp.concatenate(acts, axis=0))


def _peer(xnt, ub, vt, s2, e2, c, w):
    d, t = xnt.shape
    n = ub.shape[0]
    tb, eb = PEER_TOK_BLOCK, PEER_EXP_BLOCK
    head_spec = pl.BlockSpec((PEER_HEADS, PEER_N_KEYS, tb), lambda i, e: (0, 0, i))
    return pl.pallas_call(
        _peer_kernel,
        grid=(t // tb, n // eb),
        in_specs=[pl.BlockSpec((d, tb), lambda i, e: (0, i)),
                  pl.BlockSpec((eb, d), lambda i, e: (e, 0)),
                  pl.BlockSpec((d, eb), lambda i, e: (0, e)),
                  head_spec, head_spec, head_spec, head_spec],
        out_specs=pl.BlockSpec((d, tb), lambda i, e: (0, i)),
        out_shape=jax.ShapeDtypeStruct((d, t), F32),
        compiler_params=_params("arbitrary", "arbitrary"),
        name="peer_dense",
    )(xnt, ub, vt, s2, e2, c, w)


def _final_kernel(h_ref, pt_ref, p_ref, g_ref, wg_ref, wp_ref, gf_ref, o_ref, *, apply_final):
    h = h_ref[...] + pt_ref[...].T
    gate = jax.nn.sigmoid(_dot(_rmsnorm(h, g_ref[...]).astype(BF16), wg_ref[...]))
    h = h + gate * _dot(p_ref[...].astype(BF16), wp_ref[...])
    o_ref[...] = _rmsnorm(h, gf_ref[...]) if apply_final else h


def _final(h1, peer_t, p, g_ple, w_gate, w_proj, g_final, apply_final):
    t, d = h1.shape
    tb = TOK_BLOCK
    row = lambda i: (i, 0)
    fixed = lambda i: (0, 0)
    return pl.pallas_call(
        functools.partial(_final_kernel, apply_final=apply_final),
        grid=(t // tb,),
        in_specs=[pl.BlockSpec((tb, d), row),
                  pl.BlockSpec((d, tb), lambda i: (0, i)),
                  pl.BlockSpec((tb, p.shape[1]), row),
                  pl.BlockSpec((1, d), fixed),
                  pl.BlockSpec(w_gate.shape, fixed),
                  pl.BlockSpec(w_proj.shape, fixed),
                  pl.BlockSpec((1, d), fixed)],
        out_specs=pl.BlockSpec((tb, d), row),
        out_shape=jax.ShapeDtypeStruct((t, d), F32),
        compiler_params=_params("arbitrary"),
        name="ple_final",
    )(h1, peer_t, p, g_ple, w_gate, w_proj, g_final)


def _rope_tables(positions):
    t = positions.size
    inv_freq = 1.0 / (ROPE_THETA ** (jnp.arange(0, ROT_DIM, 2, dtype=F32) / ROT_DIM))
    ang = positions.astype(F32).reshape(t, 1) * inv_freq
    cos, sin = jnp.cos(ang), jnp.sin(ang)
    half = ROT_DIM // 2
    zeros = jnp.zeros((t, half), F32)
    rest = HEAD_DIM - ROT_DIM
    per_head = lambda a, b, fill: jnp.concatenate([a, b, jnp.full((t, rest), fill, F32)], axis=-1)
    tile = lambda a: jnp.tile(a, (1, LANES // HEAD_DIM))
    return tile(per_head(cos, cos, 1.0)), tile(per_head(zeros, sin, 0.0)), tile(per_head(-sin, zeros, 0.0))


def kernel(x, p, positions, g_mix, w_in, sinks, w_pool, pool_scale, w_out, g_ffn, w_query, sub_keys, expert_u,
           expert_v, g_ple, w_ple_gate, w_ple_proj, g_final):
    b, s, d = x.shape
    t = b * s
    depth = w_in.shape[0]
    cos, shi, slo = _rope_tables(positions)
    h = x.reshape(t, d)
    for i in range(depth):
        q, kx, vx, u = _inproj(h, g_mix[i].reshape(1, d), w_in[i].astype(BF16), cos, shi, slo)
        h = _mix(sinks[i], q, kx, vx, u, h, w_pool[i].astype(BF16), pool_scale[i].reshape(1, -1),
                 w_out[i].astype(BF16), s)
        ub, vt = _prep_experts(expert_u[i], expert_v[i])
        xnt, s2, e2, c, w = _route(h, g_ffn[i].reshape(1, d), w_query[i].astype(BF16), sub_keys[i].astype(BF16))
        peer_t = _peer(xnt, ub, vt, s2, e2, c, w)
        h = _final(h, peer_t, p[i].reshape(t, -1), g_ple[i].reshape(1, d), w_ple_gate[i].astype(BF16),
                   w_ple_proj[i].astype(BF16), g_final.reshape(1, d), i == depth - 1)
    return h.reshape(b, s, d)
```
